```python
import jax
import jax.numpy as jnp
from jax import lax
import numpy as np


D_MODEL = 2048
BATCH = 4
SEQ = 8192
DEPTH = 4
DEC_BATCH = 2
DEC_SEQ = 8192
PAST_LEN = 128

GRID_W = 64
CHUNK = 128
Q_BLOCK = 128
CONV_WIDTH = 3
D_CONV = D_MODEL // 2
D_SGU = D_MODEL // 2
SGU_GROUPS = 8
SGU_GROUP_DIM = D_SGU // SGU_GROUPS
HEAD_DIM = 128
N_Q_HEADS = D_MODEL // HEAD_DIM
N_KV_HEADS = N_Q_HEADS // 4
GQA_GROUP = N_Q_HEADS // N_KV_HEADS
D_ATTN = N_Q_HEADS * HEAD_DIM
D_KV = N_KV_HEADS * HEAD_DIM
N_BRANCH = 3
ROPE_THETA = 10000.0
ROPE_AXIS_DIM = HEAD_DIM // 2
D_FF = -(-8 * D_MODEL // (3 * 256)) * 256
SPLIT_SIZES = (D_CONV, D_CONV, D_CONV, D_SGU, D_SGU, D_ATTN, D_KV, D_KV, N_BRANCH * D_MODEL)
D_IN = 3 * D_CONV + 2 * D_SGU + D_ATTN + 2 * D_KV + N_BRANCH * D_MODEL
EPS = 1e-6

kernel_name = 'hybrid_gated_conv_sgu_axial_gqa_encoder'


def rms_norm(x, g):
    xf = x.astype(jnp.float32)
    y = xf * lax.rsqrt(jnp.mean(xf * xf, axis=-1, keepdims=True) + EPS)
    return (y * g.astype(jnp.float32)).astype(x.dtype)


def layer_norm(x, g):
    xf = x.astype(jnp.float32)
    xc = xf - jnp.mean(xf, axis=-1, keepdims=True)
    y = xc * lax.rsqrt(jnp.mean(xc * xc, axis=-1, keepdims=True) + EPS)
    return (y * g.astype(jnp.float32)).astype(x.dtype)


def split_columns(z):
    parts = []
    start = 0
    for size in SPLIT_SIZES:
        parts.append(z[..., start:start + size])
        start += size
    return parts


def short_conv_mixer(b_gate, c_gate, x_in, conv_w):
    h = c_gate * x_in
    s = h.shape[1]
    pad = CONV_WIDTH // 2
    hp = jnp.pad(h, ((0, 0), (pad, pad), (0, 0)))
    y = hp[:, 0:s] * conv_w[0]
    for j in range(1, CONV_WIDTH):
        y = y + hp[:, j:j + s] * conv_w[j]
    return b_gate * y


def spatial_gating(u, v, ln_g, w_s, b_s):
    bsz, s, _ = v.shape
    vn = layer_norm(v, ln_g).reshape(bsz, s // CHUNK, CHUNK, SGU_GROUPS, SGU_GROUP_DIM)
    mixed = jnp.einsum('gpq,bnqgc->bnpgc', w_s, vn) + b_s.T[None, None, :, :, None]
    return u * mixed.reshape(bsz, s, D_SGU)


def axial_rope_tables(seq_len):
    rows = seq_len // GRID_W
    row_idx = jnp.repeat(jnp.arange(rows, dtype=jnp.float32), GRID_W)
    col_idx = jnp.tile(jnp.arange(GRID_W, dtype=jnp.float32), rows)
    inv_freq = 1.0 / (ROPE_THETA ** (jnp.arange(0, ROPE_AXIS_DIM, 2, dtype=jnp.float32) / ROPE_AXIS_DIM))
    ang_row = row_idx[:, None] * inv_freq[None, :]
    ang_col = col_idx[:, None] * inv_freq[None, :]
    return (jnp.cos(ang_row), jnp.sin(ang_row), jnp.cos(ang_col), jnp.sin(ang_col))


def rotate_half_block(x, cos, sin):
    x1, x2 = jnp.split(x, 2, axis=-1)
    c = cos[None, :, None, :]
    sn = sin[None, :, None, :]
    return jnp.concatenate([x1 * c - x2 * sn, x2 * c + x1 * sn], axis=-1)


def apply_axial_rope(x, tables):
    cos_r, sin_r, cos_c, sin_c = tables
    xf = x.astype(jnp.float32)
    out = jnp.concatenate([rotate_half_block(xf[..., :ROPE_AXIS_DIM], cos_r, sin_r),
                           rotate_half_block(xf[..., ROPE_AXIS_DIM:], cos_c, sin_c)], axis=-1)
    return out.astype(x.dtype)


def gqa_attention(q, k, v):
    bsz, s = q.shape[0], q.shape[1]
    nb = s // Q_BLOCK
    scale = HEAD_DIM ** -0.5
    qb = (q * scale).reshape(bsz, nb, Q_BLOCK, N_KV_HEADS, GQA_GROUP, HEAD_DIM).transpose(1, 0, 2, 3, 4, 5)

    def one_block(qi):
        scores = jnp.einsum('bqhgd,bkhd->bhgqk', qi, k).astype(jnp.float32)
        p = jax.nn.softmax(scores, axis=-1).astype(v.dtype)
        return jnp.einsum('bhgqk,bkhd->bqhgd', p, v)

    out = lax.map(one_block, qb)
    return out.transpose(1, 0, 2, 3, 4, 5).reshape(bsz, s, D_ATTN)


def swiglu(h, w_in, w_out):
    gate, up = jnp.split(h @ w_in, 2, axis=-1)
    return (jax.nn.silu(gate) * up) @ w_out


def encoder_layer(x, tables, norm_mix, w_in, gate_bias, conv_w, sgu_ln, sgu_ws, sgu_b, q_norm, k_norm,
                  w_br_conv, w_br_sgu, w_br_attn, w_out, norm_ffn, w_ffn_in, w_ffn_out):
    bsz, s, _ = x.shape
    h = rms_norm(x, norm_mix)
    cb, cc, cx, su, sv, q, k, v, g = split_columns(h @ w_in)
    br_conv = short_conv_mixer(cb, cc, cx, conv_w) @ w_br_conv
    br_sgu = spatial_gating(su, sv, sgu_ln, sgu_ws, sgu_b) @ w_br_sgu
    q = apply_axial_rope(rms_norm(q.reshape(bsz, s, N_Q_HEADS, HEAD_DIM), q_norm), tables)
    k = apply_axial_rope(rms_norm(k.reshape(bsz, s, N_KV_HEADS, HEAD_DIM), k_norm), tables)
    v = v.reshape(bsz, s, N_KV_HEADS, HEAD_DIM)
    br_attn = gqa_attention(q, k, v) @ w_br_attn
    gates = jax.nn.sigmoid(g.reshape(bsz, s, N_BRANCH, D_MODEL) + gate_bias)
    merged = gates[:, :, 0] * br_conv + gates[:, :, 1] * br_sgu + gates[:, :, 2] * br_attn
    x = x + merged @ w_out
    return x + swiglu(rms_norm(x, norm_ffn), w_ffn_in, w_ffn_out)


def setup_inputs(seed: int = 0) -> dict:
    key = jax.random.key(seed)
    ks = jax.random.split(key, 20)

    def nrm(k, shape, scale):
        return jax.random.normal(k, shape, jnp.float32) * scale

    return {
        'x_prompt': nrm(ks[0], (BATCH, SEQ, D_MODEL), 1.0),
        'x_sample': nrm(ks[1], (DEC_BATCH, DEC_SEQ, D_MODEL), 1.0),
        'norm_mix': 1.0 + nrm(ks[2], (DEPTH, D_MODEL), 0.02),
        'w_in': nrm(ks[3], (DEPTH, D_MODEL, D_IN), D_MODEL ** -0.5),
        'gate_bias': nrm(ks[4], (DEPTH, N_BRANCH, D_MODEL), 0.01),
        'conv_w': nrm(ks[5], (DEPTH, CONV_WIDTH, D_CONV), CONV_WIDTH ** -0.5),
        'sgu_ln': 1.0 + nrm(ks[6], (DEPTH, D_SGU), 0.02),
        'sgu_ws': nrm(ks[7], (DEPTH, SGU_GROUPS, CHUNK, CHUNK), CHUNK ** -0.5),
        'sgu_b': 1.0 + nrm(ks[8], (DEPTH, SGU_GROUPS, CHUNK), 0.02),
        'q_norm': 1.0 + nrm(ks[9], (DEPTH, HEAD_DIM), 0.02),
        'k_norm': 1.0 + nrm(ks[10], (DEPTH, HEAD_DIM), 0.02),
        'w_br_conv': nrm(ks[11], (DEPTH, D_CONV, D_MODEL), D_CONV ** -0.5),
        'w_br_sgu': nrm(ks[12], (DEPTH, D_SGU, D_MODEL), D_SGU ** -0.5),
        'w_br_attn': nrm(ks[13], (DEPTH, D_ATTN, D_MODEL), D_ATTN ** -0.5),
        'w_out': nrm(ks[14], (DEPTH, D_MODEL, D_MODEL), D_MODEL ** -0.5),
        'norm_ffn': 1.0 + nrm(ks[15], (DEPTH, D_MODEL), 0.02),
        'w_ffn_in': nrm(ks[16], (DEPTH, D_MODEL, 2 * D_FF), D_MODEL ** -0.5),
        'w_ffn_out': nrm(ks[17], (DEPTH, D_FF, D_MODEL), D_FF ** -0.5),
        'norm_final': 1.0 + nrm(ks[18], (D_MODEL,), 0.02),
    }


def reference(x_prompt, x_sample, norm_mix, w_in, gate_bias, conv_w, sgu_ln, sgu_ws, sgu_b, q_norm, k_norm,
              w_br_conv, w_br_sgu, w_br_attn, w_out, norm_ffn, w_ffn_in, w_ffn_out, norm_final):
    def trunk(x):
        tables = axial_rope_tables(x.shape[1])
        for l in range(DEPTH):
            x = encoder_layer(x, tables, norm_mix[l], w_in[l], gate_bias[l], conv_w[l], sgu_ln[l], sgu_ws[l],
                              sgu_b[l], q_norm[l], k_norm[l], w_br_conv[l], w_br_sgu[l], w_br_attn[l],
                              w_out[l], norm_ffn[l], w_ffn_in[l], w_ffn_out[l])
        return rms_norm(x, norm_final)

    y_prompt = trunk(x_prompt)
    y_sample = trunk(x_sample)
    return (y_prompt, y_sample)
```

```python
import functools
import math

import jax
import jax.numpy as jnp
from jax import lax
from jax.experimental import pallas as pl
from jax.experimental.pallas import tpu as pltpu

_EPS = 1e-6
_GRID_W = 64
_ROPE_THETA = 10000.0
_V7X_VMEM_BYTES = 64 * 1024 * 1024
_VMEM_LIMIT_BYTES = _V7X_VMEM_BYTES - 8 * 1024 * 1024
_HALO_ROWS = 16

_F32 = jnp.float32
_BF16 = jnp.bfloat16


def _tile(n, pref, align):
    t = min(pref, n)
    t -= t % align
    while t > align and n % t:
        t -= align
    assert t >= align and n % t == 0, (n, pref, align)
    return t


def _params(*semantics):
    return pltpu.CompilerParams(dimension_semantics=semantics, vmem_limit_bytes=_VMEM_LIMIT_BYTES)


def _rms(x, gain):
    ms = jnp.mean(x * x, axis=-1, keepdims=True)
    return (x * lax.rsqrt(ms + _EPS)) * gain


def _norm_matmul_kernel(x_ref, g_ref, w_ref, o_ref, h_ref):
    @pl.when(pl.program_id(1) == 0)
    def _():
        h_ref[...] = _rms(x_ref[...], g_ref[...]).astype(h_ref.dtype)

    o_ref[...] = jnp.dot(h_ref[...], w_ref[...], preferred_element_type=_F32).astype(o_ref.dtype)


def _in_proj(x, gain, w, layer):
    t, d = x.shape
    n = w.shape[2]
    tm = _tile(t, 1024, 8)
    tn = _tile(n, 1024, 128)
    return pl.pallas_call(
        _norm_matmul_kernel,
        grid=(t // tm, n // tn),
        in_specs=[
            pl.BlockSpec((tm, d), lambda i, j: (i, 0)),
            pl.BlockSpec((None, 1, d), lambda i, j: (layer, 0, 0)),
            pl.BlockSpec((None, d, tn), lambda i, j: (layer, 0, j)),
        ],
        out_specs=pl.BlockSpec((tm, tn), lambda i, j: (i, j)),
        out_shape=jax.ShapeDtypeStruct((t, n), _BF16),
        scratch_shapes=[pltpu.VMEM((tm, d), _BF16)],
        compiler_params=_params("parallel", "arbitrary"),
        name="in_proj",
    )(x, gain, w)


def _qkv_kernel(q_ref, k_ref, v_ref, qn_ref, kn_ref, cos_ref, sin_ref, qo_ref, ko_ref, vto_ref, *, hd, qscale):
    cos = cos_ref[...]
    sin = sin_ref[...]
    lane = lax.broadcasted_iota(jnp.int32, cos.shape, 1)
    first = (lane % (hd // 2)) < (hd // 4)

    def norm_rope(x, gain):
        xn = _rms(x, gain)
        sw = jnp.where(first, pltpu.roll(xn, hd - hd // 4, 1), pltpu.roll(xn, hd // 4, 1))
        return xn * cos + sw * sin

    for g in range(q_ref.shape[1] // hd):
        x = q_ref[:, g * hd:(g + 1) * hd].astype(_F32)
        qo_ref[:, g * hd:(g + 1) * hd] = (norm_rope(x, qn_ref[...]) * qscale).astype(qo_ref.dtype)
    ko_ref[...] = norm_rope(k_ref[...].astype(_F32), kn_ref[...]).astype(ko_ref.dtype)
    vt = v_ref[...].astype(_F32).T
    tk = vto_ref.shape[2]
    for c in range(vto_ref.shape[0]):
        vto_ref[c] = vt[:, c * tk:(c + 1) * tk].astype(vto_ref.dtype)


def _qkv_prep(z, q_norm, k_norm, cos, sin, layer, *, seq, q_off, k_off, v_off, n_q, n_kv, hd, tk):
    t = z.shape[0]
    grp = n_q // n_kv
    tm = tk
    qw = grp * hd
    assert q_off % qw == 0 and k_off % hd == 0 and v_off % hd == 0 and seq % tm == 0
    tiles_per_seq = seq // tm
    qscale = (hd ** -0.5) * math.log2(math.e)
    return pl.pallas_call(
        functools.partial(_qkv_kernel, hd=hd, qscale=qscale),
        grid=(t // tm, n_kv),
        in_specs=[
            pl.BlockSpec((tm, qw), lambda i, g: (i, q_off // qw + g)),
            pl.BlockSpec((tm, hd), lambda i, g: (i, k_off // hd + g)),
            pl.BlockSpec((tm, hd), lambda i, g: (i, v_off // hd + g)),
            pl.BlockSpec((None, 1, hd), lambda i, g: (layer, 0, 0)),
            pl.BlockSpec((None, 1, hd), lambda i, g: (layer, 0, 0)),
            pl.BlockSpec((tm, hd), lambda i, g: (i % tiles_per_seq, 0)),
            pl.BlockSpec((tm, hd), lambda i, g: (i % tiles_per_seq, 0)),
        ],
        out_specs=[
            pl.BlockSpec((tm, qw), lambda i, g: (i, g)),
            pl.BlockSpec((None, tm, hd), lambda i, g: (g, i, 0)),
            pl.BlockSpec((None, tm // tk, hd, tk), lambda i, g: (g, i, 0, 0)),
        ],
        out_shape=[
            jax.ShapeDtypeStruct((t, n_q * hd), _BF16),
            jax.ShapeDtypeStruct((n_kv, t, hd), _BF16),
            jax.ShapeDtypeStruct((n_kv, t // tk, hd, tk), _BF16),
        ],
        compiler_params=_params("parallel", "parallel"),
        name="qkv_prep",
    )(z, z, z, q_norm, k_norm, cos, sin)


def _attn_kernel(q_ref, k_ref, vt_ref, o_ref, acc_ref, *, hd):
    tq = q_ref.shape[0]
    grp = q_ref.shape[1] // hd
    n_chunks, _, tk = vt_ref.shape
    n = grp * tq
    q = q_ref[...]
    qs = jnp.concatenate([q[:, g * hd:(g + 1) * hd] for g in range(grp)], axis=0)
    acc_ref[...] = jnp.zeros_like(acc_ref)

    def body(j, carry):
        m, l = carry
        kj = k_ref[pl.ds(pl.multiple_of(j * tk, tk), tk), :]
        s = lax.dot_general(kj, qs, (((1,), (1,)), ((), ())), preferred_element_type=_F32)
        m_new = jnp.maximum(m, jnp.max(s, axis=0, keepdims=True))
        alpha = jnp.exp2(m - m_new)
        p = jnp.exp2(s - m_new)
        l_new = alpha * l + jnp.sum(p, axis=0, keepdims=True)
        pv = jnp.dot(vt_ref[j], p.astype(_BF16), preferred_element_type=_F32)
        acc_ref[...] = alpha * acc_ref[...] + pv
        return m_new, l_new

    m0 = jnp.full((1, n), -jnp.inf, _F32)
    l0 = jnp.zeros((1, n), _F32)
    _, l = lax.fori_loop(0, n_chunks, body, (m0, l0))
    out = (acc_ref[...] * (1.0 / l)).T
    for g in range(grp):
        o_ref[:, g * hd:(g + 1) * hd] = out[g * tq:(g + 1) * tq, :].astype(o_ref.dtype)


def _attention(q, k, vt, *, seq, hd, tq):
    t, dq = q.shape
    n_kv, n_chunks_total, _, tk = vt.shape
    grp = dq // (n_kv * hd)
    qw = grp * hd
    batch = t // seq
    nq = seq // tq
    return pl.pallas_call(
        functools.partial(_attn_kernel, hd=hd),
        grid=(batch, n_kv, nq),
        in_specs=[
            pl.BlockSpec((tq, qw), lambda b, h, i: (b * nq + i, h)),
            pl.BlockSpec((None, seq, hd), lambda b, h, i: (h, b, 0)),
            pl.BlockSpec((None, seq // tk, hd, tk), lambda b, h, i: (h, b, 0, 0)),
        ],
        out_specs=pl.BlockSpec((tq, qw), lambda b, h, i: (b * nq + i, h)),
        out_shape=jax.ShapeDtypeStruct((t, dq), _BF16),
        scratch_shapes=[pltpu.VMEM((hd, grp * tq), _F32)],
        compiler_params=_params("parallel", "parallel", "parallel"),
        name="attention",
    )(q, k, vt)


def _mixers_kernel(cb_ref, cc_ref, cx_ref, su_ref, sv_ref, ccp_ref, cxp_ref, ccn_ref, cxn_ref,
                   cw_ref, lng_ref, ws_ref, sb_ref, conv_o, sgu_o, *, tiles_per_seq, chunk):
    i = pl.program_id(0)
    tm = cb_ref.shape[0]
    pos = i % tiles_per_seq

    h = cc_ref[...].astype(_F32) * cx_ref[...].astype(_F32)
    hp = ccp_ref[...].astype(_F32) * cxp_ref[...].astype(_F32)
    hn = ccn_ref[...].astype(_F32) * cxn_ref[...].astype(_F32)
    h_prev = jnp.where(pos != 0, hp[_HALO_ROWS - 1:_HALO_ROWS, :], 0.0)
    h_next = jnp.where(pos != tiles_per_seq - 1, hn[0:1, :], 0.0)
    row = lax.broadcasted_iota(jnp.int32, h.shape, 0)
    h_m1 = jnp.where(row == 0, h_prev, pltpu.roll(h, 1, 0))
    h_p1 = jnp.where(row == tm - 1, h_next, pltpu.roll(h, tm - 1, 0))
    w = cw_ref[...]
    y = h_m1 * w[0:1, :] + h * w[1:2, :] + h_p1 * w[2:3, :]
    conv_o[...] = (cb_ref[...].astype(_F32) * y).astype(conv_o.dtype)

    v = sv_ref[...].astype(_F32)
    vc = v - jnp.mean(v, axis=-1, keepdims=True)
    var = jnp.mean(vc * vc, axis=-1, keepdims=True)
    vn = ((vc * lax.rsqrt(var + _EPS)) * lng_ref[...]).astype(_BF16)
    n_groups = ws_ref.shape[0]
    gd = v.shape[1] // n_groups
    n_chunks = tm // chunk
    for g in range(n_groups):
        cols = slice(g * gd, (g + 1) * gd)
        xg = jnp.concatenate([vn[c * chunk:(c + 1) * chunk, cols] for c in range(n_chunks)], axis=1)
        mixed = jnp.dot(ws_ref[g], xg, preferred_element_type=_F32)
        for c in range(n_chunks):
            rows = slice(c * chunk, (c + 1) * chunk)
            blk = mixed[:, c * gd:(c + 1) * gd] + sb_ref[:, cols]
            sgu_o[rows, cols] = (su_ref[rows, cols].astype(_F32) * blk).astype(sgu_o.dtype)


def _mixers(z, conv_w, sgu_ln, sgu_ws, sgu_bias, layer, *, seq, d_conv, d_sgu):
    t = z.shape[0]
    chunk = sgu_ws.shape[2]
    n_groups = sgu_ws.shape[1]
    assert d_conv == d_sgu, "column blocks of the combined projection are indexed with one block width"
    dc = d_conv
    tm = _tile(seq, 512, max(chunk, _HALO_ROWS))
    tiles_per_seq = seq // tm
    hb = tm // _HALO_ROWS
    n_hb = t // _HALO_ROWS
    prev_map = lambda c: (lambda i: (jnp.maximum(i * hb - 1, 0), c))
    next_map = lambda c: (lambda i: (jnp.minimum((i + 1) * hb, n_hb - 1), c))
    col = lambda c: pl.BlockSpec((tm, dc), lambda i: (i, c))
    return pl.pallas_call(
        functools.partial(_mixers_kernel, tiles_per_seq=tiles_per_seq, chunk=chunk),
        grid=(t // tm,),
        in_specs=[
            col(0), col(1), col(2), col(3), col(4),
            pl.BlockSpec((_HALO_ROWS, dc), prev_map(1)),
            pl.BlockSpec((_HALO_ROWS, dc), prev_map(2)),
            pl.BlockSpec((_HALO_ROWS, dc), next_map(1)),
            pl.BlockSpec((_HALO_ROWS, dc), next_map(2)),
            pl.BlockSpec((None, conv_w.shape[1], dc), lambda i: (layer, 0, 0)),
            pl.BlockSpec((None, 1, dc), lambda i: (layer, 0, 0)),
            pl.BlockSpec((None, n_groups, chunk, chunk), lambda i: (layer, 0, 0, 0)),
            pl.BlockSpec((None, chunk, dc), lambda i: (layer, 0, 0)),
        ],
        out_specs=[pl.BlockSpec((tm, dc), lambda i: (i, 0)), pl.BlockSpec((tm, dc), lambda i: (i, 0))],
        out_shape=[jax.ShapeDtypeStruct((t, dc), _BF16), jax.ShapeDtypeStruct((t, dc), _BF16)],
        compiler_params=_params("parallel"),
        name="mixers",
    )(z, z, z, z, z, z, z, z, z, conv_w, sgu_ln, sgu_ws, sgu_bias)


def _merge_kernel(conv_ref, sgu_ref, attn_ref, g0_ref, g1_ref, g2_ref, bias_ref, wc_ref, ws_ref, wa_ref, o_ref):
    bias = bias_ref[...]

    def gate(g_ref, b):
        return jax.nn.sigmoid(g_ref[...].astype(_F32) + bias[b:b + 1, :])

    merged = gate(g0_ref, 0) * jnp.dot(conv_ref[...], wc_ref[...], preferred_element_type=_F32)
    merged = merged + gate(g1_ref, 1) * jnp.dot(sgu_ref[...], ws_ref[...], preferred_element_type=_F32)
    merged = merged + gate(g2_ref, 2) * jnp.dot(attn_ref[...], wa_ref[...], preferred_element_type=_F32)
    o_ref[...] = merged.astype(o_ref.dtype)


def _branch_merge(conv, sgu, attn, z, gate_bias, w_conv, w_sgu, w_attn, layer, *, g_off):
    t = conv.shape[0]
    d = w_conv.shape[2]
    tm = _tile(t, 1024, 8)
    tn = _tile(d, 512, 128)
    assert g_off % tn == 0 and d % tn == 0
    gate_spec = lambda b: pl.BlockSpec((tm, tn), lambda i, j: (i, (g_off + b * d) // tn + j))
    act_spec = lambda a: pl.BlockSpec((tm, a.shape[1]), lambda i, j: (i, 0))
    w_spec = lambda w: pl.BlockSpec((None, w.shape[1], tn), lambda i, j: (layer, 0, j))
    return pl.pallas_call(
        _merge_kernel,
        grid=(t // tm, d // tn),
        in_specs=[
            act_spec(conv), act_spec(sgu), act_spec(attn),
            gate_spec(0), gate_spec(1), gate_spec(2),
            pl.BlockSpec((None, gate_bias.shape[1], tn), lambda i, j: (layer, 0, j)),
            w_spec(w_conv), w_spec(w_sgu), w_spec(w_attn),
        ],
        out_specs=pl.BlockSpec((tm, tn), lambda i, j: (i, j)),
        out_shape=jax.ShapeDtypeStruct((t, d), _BF16),
        compiler_params=_params("parallel", "parallel"),
        name="branch_merge",
    )(conv, sgu, attn, z, z, z, gate_bias, w_conv, w_sgu, w_attn)


def _matmul_residual_kernel(a_ref, w_ref, r_ref, o_ref):
    o_ref[...] = r_ref[...] + jnp.dot(a_ref[...], w_ref[...], preferred_element_type=_F32)


def _matmul_residual(a, w, res, layer, *, tn_pref, name):
    t, k = a.shape
    n = w.shape[2]
    tm = _tile(t, 1024, 8)
    tn = _tile(n, tn_pref, 128)
    return pl.pallas_call(
        _matmul_residual_kernel,
        grid=(t // tm, n // tn),
        in_specs=[
            pl.BlockSpec((tm, k), lambda i, j: (i, 0)),
            pl.BlockSpec((None, k, tn), lambda i, j: (layer, 0, j)),
            pl.BlockSpec((tm, tn), lambda i, j: (i, j)),
        ],
        out_specs=pl.BlockSpec((tm, tn), lambda i, j: (i, j)),
        out_shape=jax.ShapeDtypeStruct((t, n), _F32),
        compiler_params=_params("parallel", "parallel"),
        name=name,
    )(a, w, res)


def _ffn_in_kernel(x_ref, g_ref, wg_ref, wu_ref, o_ref, h_ref):
    @pl.when(pl.program_id(1) == 0)
    def _():
        h_ref[...] = _rms(x_ref[...], g_ref[...]).astype(h_ref.dtype)

    h = h_ref[...]
    gate = jnp.dot(h, wg_ref[...], preferred_element_type=_F32)
    up = jnp.dot(h, wu_ref[...], preferred_element_type=_F32)
    o_ref[...] = ((gate * jax.nn.sigmoid(gate)) * up).astype(o_ref.dtype)


def _ffn_in(x, gain, w, layer):
    t, d = x.shape
    d_ff = w.shape[2] // 2
    tm = _tile(t, 1024, 8)
    tn = _tile(d_ff, 512, 128)
    n_tiles = d_ff // tn
    return pl.pallas_call(
        _ffn_in_kernel,
        grid=(t // tm, n_tiles),
        in_specs=[
            pl.BlockSpec((tm, d), lambda i, j: (i, 0)),
            pl.BlockSpec((None, 1, d), lambda i, j: (layer, 0, 0)),
            pl.BlockSpec((None, d, tn), lambda i, j: (layer, 0, j)),
            pl.BlockSpec((None, d, tn), lambda i, j: (layer, 0, n_tiles + j)),
        ],
        out_specs=pl.BlockSpec((tm, tn), lambda i, j: (i, j)),
        out_shape=jax.ShapeDtypeStruct((t, d_ff), _BF16),
        scratch_shapes=[pltpu.VMEM((tm, d), _BF16)],
        compiler_params=_params("parallel", "arbitrary"),
        name="ffn_in",
    )(x, gain, w, w)


def _final_norm_kernel(x_ref, g_ref, o_ref):
    o_ref[...] = _rms(x_ref[...], g_ref[...])


def _final_norm(x, gain, row_start, n_rows):
    d = x.shape[1]
    tm = _tile(math.gcd(row_start, n_rows) if row_start else n_rows, 512, 8)
    first = row_start // tm
    return pl.pallas_call(
        _final_norm_kernel,
        grid=(n_rows // tm,),
        in_specs=[pl.BlockSpec((tm, d), lambda i: (first + i, 0)), pl.BlockSpec((1, d), lambda i: (0, 0))],
        out_specs=pl.BlockSpec((tm, d), lambda i: (i, 0)),
        out_shape=jax.ShapeDtypeStruct((n_rows, d), _F32),
        compiler_params=_params("parallel"),
        name="final_norm",
    )(x, gain)


def _rope_tables(seq, hd):
    axis_dim = hd // 2
    t = jnp.arange(seq, dtype=jnp.int32)
    row_idx = (t // _GRID_W).astype(_F32)
    col_idx = (t % _GRID_W).astype(_F32)
    inv_freq = 1.0 / (_ROPE_THETA ** (jnp.arange(0, axis_dim, 2, dtype=_F32) / axis_dim))
    ang_row = row_idx[:, None] * inv_freq[None, :]
    ang_col = col_idx[:, None] * inv_freq[None, :]
    cos = jnp.concatenate([jnp.cos(ang_row)] * 2 + [jnp.cos(ang_col)] * 2, axis=-1)
    sin = jnp.concatenate([-jnp.sin(ang_row), jnp.sin(ang_row), -jnp.sin(ang_col), jnp.sin(ang_col)], axis=-1)
    return cos, sin


def _trunk(x, p, *, seq):
    d = x.shape[1]
    depth = p["w_in"].shape[0]
    d_conv = p["w_br_conv"].shape[1]
    d_sgu = p["w_br_sgu"].shape[1]
    d_attn = p["w_br_attn"].shape[1]
    hd = p["q_norm"].shape[2]
    n_branch = p["gate_bias"].shape[1]
    d_in = p["w_in"].shape[2]
    d_kv = (d_in - 3 * d_conv - 2 * d_sgu - d_attn - n_branch * d) // 2
    n_q, n_kv = d_attn // hd, d_kv // hd
    q_off = 3 * d_conv + 2 * d_sgu
    k_off = q_off + d_attn
    v_off = k_off + d_kv
    g_off = v_off + d_kv
    tk = _tile(seq, 512, 128)
    tq = _tile(seq, 256, 16)
    cos, sin = _rope_tables(seq, hd)

    for l in range(depth):
        z = _in_proj(x, p["norm_mix"], p["w_in"], l)
        q, k, vt = _qkv_prep(z, p["q_norm"], p["k_norm"], cos, sin, l, seq=seq, q_off=q_off, k_off=k_off,
                             v_off=v_off, n_q=n_q, n_kv=n_kv, hd=hd, tk=tk)
        attn = _attention(q, k, vt, seq=seq, hd=hd, tq=tq)
        conv, sgu = _mixers(z, p["conv_w"], p["sgu_ln"], p["sgu_ws"], p["sgu_bias"], l,
                            seq=seq, d_conv=d_conv, d_sgu=d_sgu)
        merged = _branch_merge(conv, sgu, attn, z, p["gate_bias"], p["w_br_conv"], p["w_br_sgu"],
                               p["w_br_attn"], l, g_off=g_off)
        x = _matmul_residual(merged, p["w_out"], x, l, tn_pref=1024, name="out_proj")
        act = _ffn_in(x, p["norm_ffn"], p["w_ffn_in"], l)
        x = _matmul_residual(act, p["w_ffn_out"], x, l, tn_pref=512, name="ffn_out")
    return x


def kernel(x_prompt, x_sample, norm_mix, w_in, gate_bias, conv_w, sgu_ln, sgu_ws, sgu_b, q_norm, k_norm,
           w_br_conv, w_br_sgu, w_br_attn, w_out, norm_ffn, w_ffn_in, w_ffn_out, norm_final):
    d = x_prompt.shape[-1]
    gd = w_br_sgu.shape[1] // sgu_ws.shape[1]
    p = {
        "norm_mix": norm_mix[:, None, :],
        "norm_ffn": norm_ffn[:, None, :],
        "q_norm": q_norm[:, None, :],
        "k_norm": k_norm[:, None, :],
        "sgu_ln": sgu_ln[:, None, :],
        "gate_bias": gate_bias,
        "conv_w": conv_w,
        "sgu_bias": jnp.repeat(jnp.swapaxes(sgu_b, 1, 2), gd, axis=2),
        "sgu_ws": sgu_ws.astype(_BF16),
        "w_in": w_in.astype(_BF16),
        "w_br_conv": w_br_conv.astype(_BF16),
        "w_br_sgu": w_br_sgu.astype(_BF16),
        "w_br_attn": w_br_attn.astype(_BF16),
        "w_out": w_out.astype(_BF16),
        "w_ffn_in": w_ffn_in.astype(_BF16),
        "w_ffn_out": w_ffn_out.astype(_BF16),
    }
    gain_final = norm_final[None, :]

    def run(xs):
        seq = xs[0].shape[1]
        x = jnp.concatenate([a.reshape(-1, d) for a in xs], axis=0) if len(xs) > 1 else xs[0].reshape(-1, d)
        x = _trunk(x, p, seq=seq)
        outs, start = [], 0
        for a in xs:
            rows = a.shape[0] * a.shape[1]
            outs.append(_final_norm(x, gain_final, start, rows).reshape(a.shape))
            start += rows
        return outs

    if x_prompt.shape[1] == x_sample.shape[1]:
        y_prompt, y_sample = run([x_prompt, x_sample])
    else:
        (y_prompt,), (y_sample,) = run([x_prompt]), run([x_sample])
    return (y_prompt, y_sample)
```

```python
import functools
import math

import jax
import jax.numpy as jnp
from jax import lax
from jax.experimental import pallas as pl
from jax.experimental.pallas import tpu as pltpu

_EPS = 1e-6
_GRID_W = 64
_ROPE_THETA = 10000.0
_V7X_VMEM_BYTES = 64 * 1024 * 1024
_VMEM_LIMIT_BYTES = _V7X_VMEM_BYTES - 8 * 1024 * 1024
_HALO_ROWS = 16

_F32 = jnp.float32
_BF16 = jnp.bfloat16


def _tile(n, pref, align):
    t = min(pref, n)
    t -= t % align
    while t > align and n % t:
        t -= align
    assert t >= align and n % t == 0, (n, pref, align)
    return t


def _params(*semantics):
    return pltpu.CompilerParams(dimension_semantics=semantics, vmem_limit_bytes=_VMEM_LIMIT_BYTES)


def _rms(x, gain):
    ms = jnp.mean(x * x, axis=-1, keepdims=True)
    return (x * lax.rsqrt(ms + _EPS)) * gain


def _norm_matmul_kernel(x_ref, g_ref, w_ref, o_ref, h_ref):
    @pl.when(pl.program_id(1) == 0)
    def _():
        h_ref[...] = _rms(x_ref[...], g_ref[...]).astype(h_ref.dtype)

    o_ref[...] = jnp.dot(h_ref[...], w_ref[...], preferred_element_type=_F32).astype(o_ref.dtype)


def _in_proj(x, gain, w, layer):
    t, d = x.shape
    n = w.shape[2]
    tm = _tile(t, 1024, 8)
    tn = _tile(n, 1024, 128)
    return pl.pallas_call(
        _norm_matmul_kernel,
        grid=(t // tm, n // tn),
        in_specs=[
            pl.BlockSpec((tm, d), lambda i, j: (i, 0)),
            pl.BlockSpec((None, 1, d), lambda i, j: (layer, 0, 0)),
            pl.BlockSpec((None, d, tn), lambda i, j: (layer, 0, j)),
        ],
        out_specs=pl.BlockSpec((tm, tn), lambda i, j: (i, j)),
        out_shape=jax.ShapeDtypeStruct((t, n), _BF16),
        scratch_shapes=[pltpu.VMEM((tm, d), _BF16)],
        compiler_params=_params("parallel", "arbitrary"),
        name="in_proj",
    )(x, gain, w)


def _qkv_kernel(q_ref, k_ref, v_ref, qn_ref, kn_ref, cos_ref, sin_ref, qo_ref, ko_ref, vto_ref, k2_ref, *, hd, qscale):
    cos = cos_ref[...]
    sin = sin_ref[...]
    lane = lax.broadcasted_iota(jnp.int32, cos.shape, 1)
    first = (lane % (hd // 2)) < (hd // 4)

    def norm_rope(x, gain):
        xn = _rms(x, gain)
        sw = jnp.where(first, pltpu.roll(xn, hd - hd // 4, 1), pltpu.roll(xn, hd // 4, 1))
        return xn * cos + sw * sin

    for g in range(q_ref.shape[1] // hd):
        x = q_ref[:, g * hd:(g + 1) * hd].astype(_F32)
        qo_ref[:, g * hd:(g + 1) * hd] = (norm_rope(x, qn_ref[...]) * qscale).astype(qo_ref.dtype)
    kb = norm_rope(k_ref[...].astype(_F32), kn_ref[...]).astype(ko_ref.dtype)
    ko_ref[...] = kb
    kf = kb.astype(_F32)
    k2 = jnp.sum(kf * kf, axis=-1, keepdims=True)
    k2_ref[...] = jnp.broadcast_to(jnp.max(k2, axis=0, keepdims=True), k2_ref.shape)
    vt = v_ref[...].astype(_F32).T
    tk = vto_ref.shape[2]
    for c in range(vto_ref.shape[0]):
        vto_ref[c] = vt[:, c * tk:(c + 1) * tk].astype(vto_ref.dtype)


def _qkv_prep(z, q_norm, k_norm, cos, sin, layer, *, seq, q_off, k_off, v_off, n_q, n_kv, hd, tk):
    t = z.shape[0]
    grp = n_q // n_kv
    tm = tk
    qw = grp * hd
    assert q_off % qw == 0 and k_off % hd == 0 and v_off % hd == 0 and seq % tm == 0
    tiles_per_seq = seq // tm
    qscale = (hd ** -0.5) * math.log2(math.e)
    return pl.pallas_call(
        functools.partial(_qkv_kernel, hd=hd, qscale=qscale),
        grid=(t // tm, n_kv),
        in_specs=[
            pl.BlockSpec((tm, qw), lambda i, g: (i, q_off // qw + g)),
            pl.BlockSpec((tm, hd), lambda i, g: (i, k_off // hd + g)),
            pl.BlockSpec((tm, hd), lambda i, g: (i, v_off // hd + g)),
            pl.BlockSpec((None, 1, hd), lambda i, g: (layer, 0, 0)),
            pl.BlockSpec((None, 1, hd), lambda i, g: (layer, 0, 0)),
            pl.BlockSpec((tm, hd), lambda i, g: (i % tiles_per_seq, 0)),
            pl.BlockSpec((tm, hd), lambda i, g: (i % tiles_per_seq, 0)),
        ],
        out_specs=[
            pl.BlockSpec((tm, qw), lambda i, g: (i, g)),
            pl.BlockSpec((None, tm, hd), lambda i, g: (g, i, 0)),
            pl.BlockSpec((None, tm // tk, hd, tk), lambda i, g: (g, i, 0, 0)),
            pl.BlockSpec((None, None, 8, 128), lambda i, g: (g, i, 0, 0)),
        ],
        out_shape=[
            jax.ShapeDtypeStruct((t, n_q * hd), _BF16),
            jax.ShapeDtypeStruct((n_kv, t, hd), _BF16),
            jax.ShapeDtypeStruct((n_kv, t // tk, hd, tk), _BF16),
            jax.ShapeDtypeStruct((n_kv, t // tm, 8, 128), _F32),
        ],
        compiler_params=_params("parallel", "parallel"),
        name="qkv_prep",
    )(z, z, z, q_norm, k_norm, cos, sin)


_EXP2_SAFE_RANGE = 60.0
_BOUND_MARGIN = 1.02


def _attn_kernel(q_ref, k_ref, vt_ref, k2_ref, o_ref, acc_ref, *, hd, tkk):
    tq = q_ref.shape[0]
    grp = q_ref.shape[1] // hd
    n_chunks, _, tk = vt_ref.shape
    n = grp * tq
    seq = k_ref.shape[0]
    nt = (((1,), (1,)), ((), ()))
    q = q_ref[...]
    qs = jnp.concatenate([q[:, g * hd:(g + 1) * hd] for g in range(grp)], axis=0)

    qf = qs.astype(_F32)
    q2 = lax.dot_general(jnp.ones((8, hd), _BF16), (qf * qf).astype(_BF16), nt, preferred_element_type=_F32)[0:1, :]
    ref = jnp.sqrt(q2 * jnp.max(k2_ref[...])) * _BOUND_MARGIN

    def finish(l):
        out = (acc_ref[...] * (1.0 / l)).T
        for g in range(grp):
            o_ref[:, g * hd:(g + 1) * hd] = out[g * tq:(g + 1) * tq, :].astype(o_ref.dtype)

    def bounded():
        acc_ref[...] = jnp.zeros_like(acc_ref)
        per = tkk // tk

        def body(j, l8):
            kj = k_ref[pl.ds(pl.multiple_of(j * tkk, tkk), tkk), :]
            p = jnp.exp2(lax.dot_general(kj, qs, nt, preferred_element_type=_F32) - ref)
            l8 = l8 + jnp.sum(p.reshape(tkk // 8, 8, n), axis=0)
            pb = p.astype(_BF16)
            pv = jnp.dot(vt_ref[j * per], pb[0:tk], preferred_element_type=_F32)
            for c in range(1, per):
                pv = pv + jnp.dot(vt_ref[j * per + c], pb[c * tk:(c + 1) * tk], preferred_element_type=_F32)
            acc_ref[...] += pv
            return l8

        l8 = lax.fori_loop(0, seq // tkk, body, jnp.zeros((8, n), _F32))
        finish(jnp.sum(l8, axis=0, keepdims=True))

    def online():
        acc_ref[...] = jnp.zeros_like(acc_ref)

        def body(j, carry):
            m, l = carry
            kj = k_ref[pl.ds(pl.multiple_of(j * tk, tk), tk), :]
            s = lax.dot_general(kj, qs, nt, preferred_element_type=_F32)
            m_new = jnp.maximum(m, jnp.max(s, axis=0, keepdims=True))
            alpha = jnp.exp2(m - m_new)
            p = jnp.exp2(s - m_new)
            l_new = alpha * l + jnp.sum(p, axis=0, keepdims=True)
            pv = jnp.dot(vt_ref[j], p.astype(_BF16), preferred_element_type=_F32)
            acc_ref[...] = alpha * acc_ref[...] + pv
            return m_new, l_new

        m0 = jnp.full((1, n), -jnp.inf, _F32)
        l0 = jnp.zeros((1, n), _F32)
        _, l = lax.fori_loop(0, n_chunks, body, (m0, l0))
        finish(l)

    lax.cond(jnp.max(ref) <= _EXP2_SAFE_RANGE, bounded, online)


def _attention(q, k, vt, k2, *, seq, hd, tq):
    t, dq = q.shape
    n_kv, _, _, tk = vt.shape
    grp = dq // (n_kv * hd)
    qw = grp * hd
    batch = t // seq
    nq = seq // tq
    tkk = _tile(seq, 2048, tk)
    k2_per_seq = k2.shape[1] // batch
    return pl.pallas_call(
        functools.partial(_attn_kernel, hd=hd, tkk=tkk),
        grid=(batch, n_kv, nq),
        in_specs=[
            pl.BlockSpec((tq, qw), lambda b, h, i: (b * nq + i, h)),
            pl.BlockSpec((None, seq, hd), lambda b, h, i: (h, b, 0)),
            pl.BlockSpec((None, seq // tk, hd, tk), lambda b, h, i: (h, b, 0, 0)),
            pl.BlockSpec((None, k2_per_seq, 8, 128), lambda b, h, i: (h, b, 0, 0)),
        ],
        out_specs=pl.BlockSpec((tq, qw), lambda b, h, i: (b * nq + i, h)),
        out_shape=jax.ShapeDtypeStruct((t, dq), _BF16),
        scratch_shapes=[pltpu.VMEM((hd, grp * tq), _F32)],
        compiler_params=_params("parallel", "parallel", "parallel"),
        name="attention",
    )(q, k, vt, k2)


def _mixers_kernel(cb_ref, cc_ref, cx_ref, su_ref, sv_ref, ccp_ref, cxp_ref, ccn_ref, cxn_ref,
                   cw_ref, lng_ref, ws_ref, sb_ref, conv_o, sgu_o, *, tiles_per_seq, chunk):
    i = pl.program_id(0)
    tm = cb_ref.shape[0]
    pos = i % tiles_per_seq

    h = cc_ref[...].astype(_F32) * cx_ref[...].astype(_F32)
    hp = ccp_ref[...].astype(_F32) * cxp_ref[...].astype(_F32)
    hn = ccn_ref[...].astype(_F32) * cxn_ref[...].astype(_F32)
    h_prev = jnp.where(pos != 0, hp[_HALO_ROWS - 1:_HALO_ROWS, :], 0.0)
    h_next = jnp.where(pos != tiles_per_seq - 1, hn[0:1, :], 0.0)
    row = lax.broadcasted_iota(jnp.int32, h.shape, 0)
    h_m1 = jnp.where(row == 0, h_prev, pltpu.roll(h, 1, 0))
    h_p1 = jnp.where(row == tm - 1, h_next, pltpu.roll(h, tm - 1, 0))
    w = cw_ref[...]
    y = h_m1 * w[0:1, :] + h * w[1:2, :] + h_p1 * w[2:3, :]
    conv_o[...] = (cb_ref[...].astype(_F32) * y).astype(conv_o.dtype)

    v = sv_ref[...].astype(_F32)
    vc = v - jnp.mean(v, axis=-1, keepdims=True)
    var = jnp.mean(vc * vc, axis=-1, keepdims=True)
    vn = ((vc * lax.rsqrt(var + _EPS)) * lng_ref[...]).astype(_BF16)
    n_groups = ws_ref.shape[0]
    gd = v.shape[1] // n_groups
    n_chunks = tm // chunk
    for g in range(n_groups):
        cols = slice(g * gd, (g + 1) * gd)
        xg = jnp.concatenate([vn[c * chunk:(c + 1) * chunk, cols] for c in range(n_chunks)], axis=1)
        mixed = jnp.dot(ws_ref[g], xg, preferred_element_type=_F32)
        for c in range(n_chunks):
            rows = slice(c * chunk, (c + 1) * chunk)
            blk = mixed[:, c * gd:(c + 1) * gd] + sb_ref[:, cols]
            sgu_o[rows, cols] = (su_ref[rows, cols].astype(_F32) * blk).astype(sgu_o.dtype)


def _mixers(z, conv_w, sgu_ln, sgu_ws, sgu_bias, layer, *, seq, d_conv, d_sgu):
    t = z.shape[0]
    chunk = sgu_ws.shape[2]
    n_groups = sgu_ws.shape[1]
    assert d_conv == d_sgu, "column blocks of the combined projection are indexed with one block width"
    dc = d_conv
    tm = _tile(seq, 512, max(chunk, _HALO_ROWS))
    tiles_per_seq = seq // tm
    hb = tm // _HALO_ROWS
    n_hb = t // _HALO_ROWS
    prev_map = lambda c: (lambda i: (jnp.maximum(i * hb - 1, 0), c))
    next_map = lambda c: (lambda i: (jnp.minimum((i + 1) * hb, n_hb - 1), c))
    col = lambda c: pl.BlockSpec((tm, dc), lambda i: (i, c))
    return pl.pallas_call(
        functools.partial(_mixers_kernel, tiles_per_seq=tiles_per_seq, chunk=chunk),
        grid=(t // tm,),
        in_specs=[
            col(0), col(1), col(2), col(3), col(4),
            pl.BlockSpec((_HALO_ROWS, dc), prev_map(1)),
            pl.BlockSpec((_HALO_ROWS, dc), prev_map(2)),
            pl.BlockSpec((_HALO_ROWS, dc), next_map(1)),
            pl.BlockSpec((_HALO_ROWS, dc), next_map(2)),
            pl.BlockSpec((None, conv_w.shape[1], dc), lambda i: (layer, 0, 0)),
            pl.BlockSpec((None, 1, dc), lambda i: (layer, 0, 0)),
            pl.BlockSpec((None, n_groups, chunk, chunk), lambda i: (layer, 0, 0, 0)),
            pl.BlockSpec((None, chunk, dc), lambda i: (layer, 0, 0)),
        ],
        out_specs=[pl.BlockSpec((tm, dc), lambda i: (i, 0)), pl.BlockSpec((tm, dc), lambda i: (i, 0))],
        out_shape=[jax.ShapeDtypeStruct((t, dc), _BF16), jax.ShapeDtypeStruct((t, dc), _BF16)],
        compiler_params=_params("parallel"),
        name="mixers",
    )(z, z, z, z, z, z, z, z, z, conv_w, sgu_ln, sgu_ws, sgu_bias)


def _merge_kernel(conv_ref, sgu_ref, attn_ref, g0_ref, g1_ref, g2_ref, bias_ref, wc_ref, ws_ref, wa_ref, o_ref):
    bias = bias_ref[...]

    def gate(g_ref, b):
        return jax.nn.sigmoid(g_ref[...].astype(_F32) + bias[b:b + 1, :])

    merged = gate(g0_ref, 0) * jnp.dot(conv_ref[...], wc_ref[...], preferred_element_type=_F32)
    merged = merged + gate(g1_ref, 1) * jnp.dot(sgu_ref[...], ws_ref[...], preferred_element_type=_F32)
    merged = merged + gate(g2_ref, 2) * jnp.dot(attn_ref[...], wa_ref[...], preferred_element_type=_F32)
    o_ref[...] = merged.astype(o_ref.dtype)


def _branch_merge(conv, sgu, attn, z, gate_bias, w_conv, w_sgu, w_attn, layer, *, g_off):
    t = conv.shape[0]
    d = w_conv.shape[2]
    tm = _tile(t, 1024, 8)
    tn = _tile(d, 512, 128)
    assert g_off % tn == 0 and d % tn == 0
    gate_spec = lambda b: pl.BlockSpec((tm, tn), lambda i, j: (i, (g_off + b * d) // tn + j))
    act_spec = lambda a: pl.BlockSpec((tm, a.shape[1]), lambda i, j: (i, 0))
    w_spec = lambda w: pl.BlockSpec((None, w.shape[1], tn), lambda i, j: (layer, 0, j))
    return pl.pallas_call(
        _merge_kernel,
        grid=(t // tm, d // tn),
        in_specs=[
            act_spec(conv), act_spec(sgu), act_spec(attn),
            gate_spec(0), gate_spec(1), gate_spec(2),
            pl.BlockSpec((None, gate_bias.shape[1], tn), lambda i, j: (layer, 0, j)),
            w_spec(w_conv), w_spec(w_sgu), w_spec(w_attn),
        ],
        out_specs=pl.BlockSpec((tm, tn), lambda i, j: (i, j)),
        out_shape=jax.ShapeDtypeStruct((t, d), _BF16),
        compiler_params=_params("parallel", "parallel"),
        name="branch_merge",
    )(conv, sgu, attn, z, z, z, gate_bias, w_conv, w_sgu, w_attn)


def _matmul_residual_kernel(a_ref, w_ref, r_ref, o_ref):
    o_ref[...] = r_ref[...] + jnp.dot(a_ref[...], w_ref[...], preferred_element_type=_F32)


def _matmul_residual(a, w, res, layer, *, tn_pref, name):
    t, k = a.shape
    n = w.shape[2]
    tm = _tile(t, 1024, 8)
    tn = _tile(n, tn_pref, 128)
    return pl.pallas_call(
        _matmul_residual_kernel,
        grid=(t // tm, n // tn),
        in_specs=[
            pl.BlockSpec((tm, k), lambda i, j: (i, 0)),
            pl.BlockSpec((None, k, tn), lambda i, j: (layer, 0, j)),
            pl.BlockSpec((tm, tn), lambda i, j: (i, j)),
        ],
        out_specs=pl.BlockSpec((tm, tn), lambda i, j: (i, j)),
        out_shape=jax.ShapeDtypeStruct((t, n), _F32),
        compiler_params=_params("parallel", "parallel"),
        name=name,
    )(a, w, res)


def _ffn_in_kernel(x_ref, g_ref, wg_ref, wu_ref, o_ref, h_ref):
    @pl.when(pl.program_id(1) == 0)
    def _():
        h_ref[...] = _rms(x_ref[...], g_ref[...]).astype(h_ref.dtype)

    h = h_ref[...]
    gate = jnp.dot(h, wg_ref[...], preferred_element_type=_F32)
    up = jnp.dot(h, wu_ref[...], preferred_element_type=_F32)
    o_ref[...] = ((gate * jax.nn.sigmoid(gate)) * up).astype(o_ref.dtype)


def _ffn_in(x, gain, w, layer):
    t, d = x.shape
    d_ff = w.shape[2] // 2
    tm = _tile(t, 1024, 8)
    tn = _tile(d_ff, 512, 128)
    n_tiles = d_ff // tn
    return pl.pallas_call(
        _ffn_in_kernel,
        grid=(t // tm, n_tiles),
        in_specs=[
            pl.BlockSpec((tm, d), lambda i, j: (i, 0)),
            pl.BlockSpec((None, 1, d), lambda i, j: (layer, 0, 0)),
            pl.BlockSpec((None, d, tn), lambda i, j: (layer, 0, j)),
            pl.BlockSpec((None, d, tn), lambda i, j: (layer, 0, n_tiles + j)),
        ],
        out_specs=pl.BlockSpec((tm, tn), lambda i, j: (i, j)),
        out_shape=jax.ShapeDtypeStruct((t, d_ff), _BF16),
        scratch_shapes=[pltpu.VMEM((tm, d), _BF16)],
        compiler_params=_params("parallel", "arbitrary"),
        name="ffn_in",
    )(x, gain, w, w)


def _final_norm_kernel(x_ref, g_ref, o_ref):
    o_ref[...] = _rms(x_ref[...], g_ref[...])


def _final_norm(x, gain, row_start, n_rows):
    d = x.shape[1]
    tm = _tile(math.gcd(row_start, n_rows) if row_start else n_rows, 512, 8)
    first = row_start // tm
    return pl.pallas_call(
        _final_norm_kernel,
        grid=(n_rows // tm,),
        in_specs=[pl.BlockSpec((tm, d), lambda i: (first + i, 0)), pl.BlockSpec((1, d), lambda i: (0, 0))],
        out_specs=pl.BlockSpec((tm, d), lambda i: (i, 0)),
        out_shape=jax.ShapeDtypeStruct((n_rows, d), _F32),
        compiler_params=_params("parallel"),
        name="final_norm",
    )(x, gain)


def _rope_tables(seq, hd):
    axis_dim = hd // 2
    t = jnp.arange(seq, dtype=jnp.int32)
    row_idx = (t // _GRID_W).astype(_F32)
    col_idx = (t % _GRID_W).astype(_F32)
    inv_freq = 1.0 / (_ROPE_THETA ** (jnp.arange(0, axis_dim, 2, dtype=_F32) / axis_dim))
    ang_row = row_idx[:, None] * inv_freq[None, :]
    ang_col = col_idx[:, None] * inv_freq[None, :]
    cos = jnp.concatenate([jnp.cos(ang_row)] * 2 + [jnp.cos(ang_col)] * 2, axis=-1)
    sin = jnp.concatenate([-jnp.sin(ang_row), jnp.sin(ang_row), -jnp.sin(ang_col), jnp.sin(ang_col)], axis=-1)
    return cos, sin


def _trunk(x, p, *, seq):
    d = x.shape[1]
    depth = p["w_in"].shape[0]
    d_conv = p["w_br_conv"].shape[1]
    d_sgu = p["w_br_sgu"].shape[1]
    d_attn = p["w_br_attn"].shape[1]
    hd = p["q_norm"].shape[2]
    n_branch = p["gate_bias"].shape[1]
    d_in = p["w_in"].shape[2]
    d_kv = (d_in - 3 * d_conv - 2 * d_sgu - d_attn - n_branch * d) // 2
    n_q, n_kv = d_attn // hd, d_kv // hd
    q_off = 3 * d_conv + 2 * d_sgu
    k_off = q_off + d_attn
    v_off = k_off + d_kv
    g_off = v_off + d_kv
    tk = _tile(seq, 512, 128)
    tq = _tile(seq, 512, 16)
    cos, sin = _rope_tables(seq, hd)

    for l in range(depth):
        z = _in_proj(x, p["norm_mix"], p["w_in"], l)
        q, k, vt, k2 = _qkv_prep(z, p["q_norm"], p["k_norm"], cos, sin, l, seq=seq, q_off=q_off, k_off=k_off,
                                 v_off=v_off, n_q=n_q, n_kv=n_kv, hd=hd, tk=tk)
        attn = _attention(q, k, vt, k2, seq=seq, hd=hd, tq=tq)
        conv, sgu = _mixers(z, p["conv_w"], p["sgu_ln"], p["sgu_ws"], p["sgu_bias"], l,
                            seq=seq, d_conv=d_conv, d_sgu=d_sgu)
        merged = _branch_merge(conv, sgu, attn, z, p["gate_bias"], p["w_br_conv"], p["w_br_sgu"],
                               p["w_br_attn"], l, g_off=g_off)
        x = _matmul_residual(merged, p["w_out"], x, l, tn_pref=1024, name="out_proj")
        act = _ffn_in(x, p["norm_ffn"], p["w_ffn_in"], l)
        x = _matmul_residual(act, p["w_ffn_out"], x, l, tn_pref=512, name="ffn_out")
    return x


def kernel(x_prompt, x_sample, norm_mix, w_in, gate_bias, conv_w, sgu_ln, sgu_ws, sgu_b, q_norm, k_norm,
           w_br_conv, w_br_sgu, w_br_attn, w_out, norm_ffn, w_ffn_in, w_ffn_out, norm_final):
    d = x_prompt.shape[-1]
    gd = w_br_sgu.shape[1] // sgu_ws.shape[1]
    p = {
        "norm_mix": norm_mix[:, None, :],
        "norm_ffn": norm_ffn[:, None, :],
        "q_norm": q_norm[:, None, :],
        "k_norm": k_norm[:, None, :],
        "sgu_ln": sgu_ln[:, None, :],
        "gate_bias": gate_bias,
        "conv_w": conv_w,
        "sgu_bias": jnp.repeat(jnp.swapaxes(sgu_b, 1, 2), gd, axis=2),
        "sgu_ws": sgu_ws.astype(_BF16),
        "w_in": w_in.astype(_BF16),
        "w_br_conv": w_br_conv.astype(_BF16),
        "w_br_sgu": w_br_sgu.astype(_BF16),
        "w_br_attn": w_br_attn.astype(_BF16),
        "w_out": w_out.astype(_BF16),
        "w_ffn_in": w_ffn_in.astype(_BF16),
        "w_ffn_out": w_ffn_out.astype(_BF16),
    }
    gain_final = norm_final[None, :]

    def run(xs):
        seq = xs[0].shape[1]
        x = jnp.concatenate([a.reshape(-1, d) for a in xs], axis=0) if len(xs) > 1 else xs[0].reshape(-1, d)
        x = _trunk(x, p, seq=seq)
        outs, start = [], 0
        for a in xs:
            rows = a.shape[0] * a.shape[1]
            outs.append(_final_norm(x, gain_final, start, rows).reshape(a.shape))
            start += rows
        return outs

    if x_prompt.shape[1] == x_sample.shape[1]:
        y_prompt, y_sample = run([x_prompt, x_sample])
    else:
        (y_prompt,), (y_sample,) = run([x_prompt]), run([x_sample])
    return (y_prompt, y_sample)
```

```python
import functools
import math

import jax
import jax.numpy as jnp
from jax import lax
from jax.experimental import pallas as pl
from jax.experimental.pallas import tpu as pltpu

_EPS = 1e-6
_GRID_W = 64
_ROPE_THETA = 10000.0
_V7X_VMEM_BYTES = 64 * 1024 * 1024
_VMEM_LIMIT_BYTES = _V7X_VMEM_BYTES - 8 * 1024 * 1024
_HALO_ROWS = 16

_F32 = jnp.float32
_BF16 = jnp.bfloat16


def _tile(n, pref, align):
    t = min(pref, n)
    t -= t % align
    while t > align and n % t:
        t -= align
    assert t >= align and n % t == 0, (n, pref, align)
    return t


def _params(*semantics):
    return pltpu.CompilerParams(dimension_semantics=semantics, vmem_limit_bytes=_VMEM_LIMIT_BYTES)


def _rms(x, gain):
    ms = jnp.mean(x * x, axis=-1, keepdims=True)
    return (x * lax.rsqrt(ms + _EPS)) * gain


def _norm_matmul_kernel(x_ref, g_ref, w_ref, o_ref, h_ref):
    @pl.when(pl.program_id(1) == 0)
    def _():
        h_ref[...] = _rms(x_ref[...], g_ref[...]).astype(h_ref.dtype)

    o_ref[...] = jnp.dot(h_ref[...], w_ref[...], preferred_element_type=_F32).astype(o_ref.dtype)


def _in_proj(x, gain, w, layer):
    t, d = x.shape
    n = w.shape[2]
    tm = _tile(t, 1024, 8)
    tn = _tile(n, 2048, 128)
    return pl.pallas_call(
        _norm_matmul_kernel,
        grid=(t // tm, n // tn),
        in_specs=[
            pl.BlockSpec((tm, d), lambda i, j: (i, 0)),
            pl.BlockSpec((None, 1, d), lambda i, j: (layer, 0, 0)),
            pl.BlockSpec((None, d, tn), lambda i, j: (layer, 0, j)),
        ],
        out_specs=pl.BlockSpec((tm, tn), lambda i, j: (i, j)),
        out_shape=jax.ShapeDtypeStruct((t, n), _BF16),
        scratch_shapes=[pltpu.VMEM((tm, d), _BF16)],
        compiler_params=_params("parallel", "arbitrary"),
        name="in_proj",
    )(x, gain, w)


def _qkv_kernel(q_ref, k_ref, v_ref, qg_ref, kg_ref, cos_ref, sin_ref, perm_ref, qo_ref, ko_ref, vto_ref, *, hd):
    ones = jnp.ones((hd, hd), _BF16)
    perm = perm_ref[...]
    cos = cos_ref[...]
    sin = sin_ref[...]

    def norm_rope(g_ref):
        a = cos * g_ref[0:1, :]
        b = sin * g_ref[1:2, :]

        def apply(xb):
            x = xb.astype(_F32)
            partner = jnp.dot(xb, perm, preferred_element_type=_F32)
            ms = jnp.dot((x * x).astype(_BF16), ones, preferred_element_type=_F32) * (1.0 / hd)
            return (x * a + partner * b) * lax.rsqrt(ms + _EPS)

        return apply

    q_rope = norm_rope(qg_ref)
    for g in range(q_ref.shape[1] // hd):
        cols = slice(g * hd, (g + 1) * hd)
        qo_ref[:, cols] = q_rope(q_ref[:, cols]).astype(qo_ref.dtype)
    ko_ref[...] = norm_rope(kg_ref)(k_ref[...]).astype(ko_ref.dtype)
    vt = v_ref[...].astype(_F32).T
    tk = vto_ref.shape[2]
    for c in range(vto_ref.shape[0]):
        vto_ref[c] = vt[:, c * tk:(c + 1) * tk].astype(vto_ref.dtype)


def _rotary_partner(hd):
    lane = jnp.arange(hd)
    return jnp.where((lane % (hd // 2)) < (hd // 4), lane + hd // 4, lane - hd // 4)


def _qkv_prep(z, q_gain, k_gain, cos, sin, layer, *, seq, q_off, k_off, v_off, n_q, n_kv, hd, tk):
    t = z.shape[0]
    grp = n_q // n_kv
    tm = tk
    qw = grp * hd
    assert q_off % qw == 0 and k_off % hd == 0 and v_off % hd == 0 and seq % tm == 0
    tiles_per_seq = seq // tm
    perm = (jnp.arange(hd)[:, None] == _rotary_partner(hd)[None, :]).astype(_BF16)
    return pl.pallas_call(
        functools.partial(_qkv_kernel, hd=hd),
        grid=(t // tm, n_kv),
        in_specs=[
            pl.BlockSpec((tm, qw), lambda i, g: (i, q_off // qw + g)),
            pl.BlockSpec((tm, hd), lambda i, g: (i, k_off // hd + g)),
            pl.BlockSpec((tm, hd), lambda i, g: (i, v_off // hd + g)),
            pl.BlockSpec((None, 2, hd), lambda i, g: (layer, 0, 0)),
            pl.BlockSpec((None, 2, hd), lambda i, g: (layer, 0, 0)),
            pl.BlockSpec((tm, hd), lambda i, g: (i % tiles_per_seq, 0)),
            pl.BlockSpec((tm, hd), lambda i, g: (i % tiles_per_seq, 0)),
            pl.BlockSpec((hd, hd), lambda i, g: (0, 0)),
        ],
        out_specs=[
            pl.BlockSpec((tm, qw), lambda i, g: (i, g)),
            pl.BlockSpec((None, tm, hd), lambda i, g: (g, i, 0)),
            pl.BlockSpec((None, tm // tk, hd, tk), lambda i, g: (g, i, 0, 0)),
        ],
        out_shape=[
            jax.ShapeDtypeStruct((t, n_q * hd), _BF16),
            jax.ShapeDtypeStruct((n_kv, t, hd), _BF16),
            jax.ShapeDtypeStruct((n_kv, t // tk, hd, tk), _BF16),
        ],
        compiler_params=_params("parallel", "parallel"),
        name="qkv_prep",
    )(z, z, z, q_gain, k_gain, cos, sin, perm)


_EXP2_SAFE_RANGE = 60.0
_BOUND_MARGIN = 1.02


def _score_bound(q_gain, k_gain, hd):
    gq = jnp.max(jnp.abs(q_gain[:, 0, :]), axis=-1)
    gk = jnp.max(jnp.abs(k_gain[:, 0, :]), axis=-1)
    return (hd * _BOUND_MARGIN) * gq * gk


def _attn_kernel(bound_ref, q_ref, k_ref, vt_ref, o_ref, acc_ref, *, hd, tkk, layer):
    tq = q_ref.shape[0]
    grp = q_ref.shape[1] // hd
    n_chunks, _, tk = vt_ref.shape
    n = grp * tq
    seq = k_ref.shape[0]
    nt = (((1,), (1,)), ((), ()))
    q = q_ref[...]
    qs = jnp.concatenate([q[:, g * hd:(g + 1) * hd] for g in range(grp)], axis=0)
    ref = bound_ref[layer]

    def finish(l):
        out = (acc_ref[...] * (1.0 / l)).T
        for g in range(grp):
            o_ref[:, g * hd:(g + 1) * hd] = out[g * tq:(g + 1) * tq, :].astype(o_ref.dtype)

    def bounded():
        per = tkk // tk

        def chunk(j):
            kj = k_ref[pl.ds(pl.multiple_of(j * tkk, tkk), tkk), :]
            p = jnp.exp2(lax.dot_general(kj, qs, nt, preferred_element_type=_F32) - ref)
            pb = p.astype(_BF16)
            pv = jnp.dot(vt_ref[j * per], pb[0:tk], preferred_element_type=_F32)
            for c in range(1, per):
                pv = pv + jnp.dot(vt_ref[j * per + c], pb[c * tk:(c + 1) * tk], preferred_element_type=_F32)
            return jnp.sum(p.reshape(tkk // 8, 8, n), axis=0), pv

        l8, pv = chunk(0)
        acc_ref[...] = pv

        def body(j, l8):
            l8_j, pv = chunk(j)
            acc_ref[...] += pv
            return l8 + l8_j

        l8 = lax.fori_loop(1, seq // tkk, body, l8)
        finish(jnp.sum(l8, axis=0, keepdims=True))

    def online():
        acc_ref[...] = jnp.zeros_like(acc_ref)

        def body(j, carry):
            m, l = carry
            kj = k_ref[pl.ds(pl.multiple_of(j * tk, tk), tk), :]
            s = lax.dot_general(kj, qs, nt, preferred_element_type=_F32)
            m_new = jnp.maximum(m, jnp.max(s, axis=0, keepdims=True))
            alpha = jnp.exp2(m - m_new)
            p = jnp.exp2(s - m_new)
            l_new = alpha * l + jnp.sum(p, axis=0, keepdims=True)
            pv = jnp.dot(vt_ref[j], p.astype(_BF16), preferred_element_type=_F32)
            acc_ref[...] = alpha * acc_ref[...] + pv
            return m_new, l_new

        m0 = jnp.full((1, n), -jnp.inf, _F32)
        l0 = jnp.zeros((1, n), _F32)
        _, l = lax.fori_loop(0, n_chunks, body, (m0, l0))
        finish(l)

    lax.cond(ref <= _EXP2_SAFE_RANGE, bounded, online)


def _attention(bound, q, k, vt, layer, *, seq, hd, tq):
    t, dq = q.shape
    n_kv, _, _, tk = vt.shape
    grp = dq // (n_kv * hd)
    qw = grp * hd
    batch = t // seq
    nq = seq // tq
    tkk = _tile(seq, 2048, tk)
    return pl.pallas_call(
        functools.partial(_attn_kernel, hd=hd, tkk=tkk, layer=layer),
        grid=(batch, n_kv, nq),
        in_specs=[
            pl.BlockSpec(memory_space=pltpu.SMEM),
            pl.BlockSpec((tq, qw), lambda b, h, i: (b * nq + i, h)),
            pl.BlockSpec((None, seq, hd), lambda b, h, i: (h, b, 0)),
            pl.BlockSpec((None, seq // tk, hd, tk), lambda b, h, i: (h, b, 0, 0)),
        ],
        out_specs=pl.BlockSpec((tq, qw), lambda b, h, i: (b * nq + i, h)),
        out_shape=jax.ShapeDtypeStruct((t, dq), _BF16),
        scratch_shapes=[pltpu.VMEM((hd, grp * tq), _F32)],
        compiler_params=_params("parallel", "parallel", "parallel"),
        name="attention",
    )(bound, q, k, vt)


def _mixers_kernel(cb_ref, cc_ref, cx_ref, su_ref, sv_ref, ccp_ref, cxp_ref, ccn_ref, cxn_ref,
                   cw_ref, lng_ref, ws_ref, sb_ref, conv_o, sgu_o, *, tiles_per_seq, chunk):
    i = pl.program_id(0)
    tm = cb_ref.shape[0]
    pos = i % tiles_per_seq

    h = cc_ref[...].astype(_F32) * cx_ref[...].astype(_F32)
    hp = ccp_ref[...].astype(_F32) * cxp_ref[...].astype(_F32)
    hn = ccn_ref[...].astype(_F32) * cxn_ref[...].astype(_F32)
    h_prev = jnp.where(pos != 0, hp[_HALO_ROWS - 1:_HALO_ROWS, :], 0.0)
    h_next = jnp.where(pos != tiles_per_seq - 1, hn[0:1, :], 0.0)
    row = lax.broadcasted_iota(jnp.int32, h.shape, 0)
    h_m1 = jnp.where(row == 0, h_prev, pltpu.roll(h, 1, 0))
    h_p1 = jnp.where(row == tm - 1, h_next, pltpu.roll(h, tm - 1, 0))
    w = cw_ref[...]
    y = h_m1 * w[0:1, :] + h * w[1:2, :] + h_p1 * w[2:3, :]
    conv_o[...] = (cb_ref[...].astype(_F32) * y).astype(conv_o.dtype)

    v = sv_ref[...].astype(_F32)
    vc = v - jnp.mean(v, axis=-1, keepdims=True)
    var = jnp.mean(vc * vc, axis=-1, keepdims=True)
    vn = ((vc * lax.rsqrt(var + _EPS)) * lng_ref[...]).astype(_BF16)
    n_groups = ws_ref.shape[0]
    gd = v.shape[1] // n_groups
    n_chunks = tm // chunk
    for g in range(n_groups):
        cols = slice(g * gd, (g + 1) * gd)
        xg = jnp.concatenate([vn[c * chunk:(c + 1) * chunk, cols] for c in range(n_chunks)], axis=1)
        mixed = jnp.dot(ws_ref[g], xg, preferred_element_type=_F32)
        for c in range(n_chunks):
            rows = slice(c * chunk, (c + 1) * chunk)
            blk = mixed[:, c * gd:(c + 1) * gd] + sb_ref[:, cols]
            sgu_o[rows, cols] = (su_ref[rows, cols].astype(_F32) * blk).astype(sgu_o.dtype)


def _mixers(z, conv_w, sgu_ln, sgu_ws, sgu_bias, layer, *, seq, d_conv, d_sgu):
    t = z.shape[0]
    chunk = sgu_ws.shape[2]
    n_groups = sgu_ws.shape[1]
    assert d_conv == d_sgu, "column blocks of the combined projection are indexed with one block width"
    dc = d_conv
    tm = _tile(seq, 512, max(chunk, _HALO_ROWS))
    tiles_per_seq = seq // tm
    hb = tm // _HALO_ROWS
    n_hb = t // _HALO_ROWS
    prev_map = lambda c: (lambda i: (jnp.maximum(i * hb - 1, 0), c))
    next_map = lambda c: (lambda i: (jnp.minimum((i + 1) * hb, n_hb - 1), c))
    col = lambda c: pl.BlockSpec((tm, dc), lambda i: (i, c))
    return pl.pallas_call(
        functools.partial(_mixers_kernel, tiles_per_seq=tiles_per_seq, chunk=chunk),
        grid=(t // tm,),
        in_specs=[
            col(0), col(1), col(2), col(3), col(4),
            pl.BlockSpec((_HALO_ROWS, dc), prev_map(1)),
            pl.BlockSpec((_HALO_ROWS, dc), prev_map(2)),
            pl.BlockSpec((_HALO_ROWS, dc), next_map(1)),
            pl.BlockSpec((_HALO_ROWS, dc), next_map(2)),
            pl.BlockSpec((None, conv_w.shape[1], dc), lambda i: (layer, 0, 0)),
            pl.BlockSpec((None, 1, dc), lambda i: (layer, 0, 0)),
            pl.BlockSpec((None, n_groups, chunk, chunk), lambda i: (layer, 0, 0, 0)),
            pl.BlockSpec((None, chunk, dc), lambda i: (layer, 0, 0)),
        ],
        out_specs=[pl.BlockSpec((tm, dc), lambda i: (i, 0)), pl.BlockSpec((tm, dc), lambda i: (i, 0))],
        out_shape=[jax.ShapeDtypeStruct((t, dc), _BF16), jax.ShapeDtypeStruct((t, dc), _BF16)],
        compiler_params=_params("parallel"),
        name="mixers",
    )(z, z, z, z, z, z, z, z, z, conv_w, sgu_ln, sgu_ws, sgu_bias)


def _merge_kernel(conv_ref, sgu_ref, attn_ref, g0_ref, g1_ref, g2_ref, bias_ref, wc_ref, ws_ref, wa_ref, o_ref):
    bias = bias_ref[...]

    def gate(g_ref, b):
        return jax.nn.sigmoid(g_ref[...].astype(_F32) + bias[b:b + 1, :])

    merged = gate(g0_ref, 0) * jnp.dot(conv_ref[...], wc_ref[...], preferred_element_type=_F32)
    merged = merged + gate(g1_ref, 1) * jnp.dot(sgu_ref[...], ws_ref[...], preferred_element_type=_F32)
    merged = merged + gate(g2_ref, 2) * jnp.dot(attn_ref[...], wa_ref[...], preferred_element_type=_F32)
    o_ref[...] = merged.astype(o_ref.dtype)


def _branch_merge(conv, sgu, attn, z, gate_bias, w_conv, w_sgu, w_attn, layer, *, g_off):
    t = conv.shape[0]
    d = w_conv.shape[2]
    tm = _tile(t, 1024, 8)
    tn = _tile(d, 512, 128)
    assert g_off % tn == 0 and d % tn == 0
    gate_spec = lambda b: pl.BlockSpec((tm, tn), lambda i, j: (i, (g_off + b * d) // tn + j))
    act_spec = lambda a: pl.BlockSpec((tm, a.shape[1]), lambda i, j: (i, 0))
    w_spec = lambda w: pl.BlockSpec((None, w.shape[1], tn), lambda i, j: (layer, 0, j))
    return pl.pallas_call(
        _merge_kernel,
        grid=(t // tm, d // tn),
        in_specs=[
            act_spec(conv), act_spec(sgu), act_spec(attn),
            gate_spec(0), gate_spec(1), gate_spec(2),
            pl.BlockSpec((None, gate_bias.shape[1], tn), lambda i, j: (layer, 0, j)),
            w_spec(w_conv), w_spec(w_sgu), w_spec(w_attn),
        ],
        out_specs=pl.BlockSpec((tm, tn), lambda i, j: (i, j)),
        out_shape=jax.ShapeDtypeStruct((t, d), _BF16),
        compiler_params=_params("parallel", "parallel"),
        name="branch_merge",
    )(conv, sgu, attn, z, z, z, gate_bias, w_conv, w_sgu, w_attn)


def _matmul_residual_kernel(a_ref, w_ref, r_ref, o_ref):
    o_ref[...] = r_ref[...] + jnp.dot(a_ref[...], w_ref[...], preferred_element_type=_F32)


def _matmul_residual(a, w, res, layer, *, tn_pref, name):
    t, k = a.shape
    n = w.shape[2]
    tm = _tile(t, 1024, 8)
    tn = _tile(n, tn_pref, 128)
    return pl.pallas_call(
        _matmul_residual_kernel,
        grid=(t // tm, n // tn),
        in_specs=[
            pl.BlockSpec((tm, k), lambda i, j: (i, 0)),
            pl.BlockSpec((None, k, tn), lambda i, j: (layer, 0, j)),
            pl.BlockSpec((tm, tn), lambda i, j: (i, j)),
        ],
        out_specs=pl.BlockSpec((tm, tn), lambda i, j: (i, j)),
        out_shape=jax.ShapeDtypeStruct((t, n), _F32),
        compiler_params=_params("parallel", "parallel"),
        name=name,
    )(a, w, res)


def _ffn_in_kernel(x_ref, g_ref, wg_ref, wu_ref, o_ref, h_ref):
    @pl.when(pl.program_id(1) == 0)
    def _():
        h_ref[...] = _rms(x_ref[...], g_ref[...]).astype(h_ref.dtype)

    h = h_ref[...]
    gate = jnp.dot(h, wg_ref[...], preferred_element_type=_F32)
    up = jnp.dot(h, wu_ref[...], preferred_element_type=_F32)
    o_ref[...] = ((gate * jax.nn.sigmoid(gate)) * up).astype(o_ref.dtype)


def _ffn_in(x, gain, w, layer):
    t, d = x.shape
    d_ff = w.shape[2] // 2
    tm = _tile(t, 1024, 8)
    tn = _tile(d_ff, 512, 128)
    n_tiles = d_ff // tn
    return pl.pallas_call(
        _ffn_in_kernel,
        grid=(t // tm, n_tiles),
        in_specs=[
            pl.BlockSpec((tm, d), lambda i, j: (i, 0)),
            pl.BlockSpec((None, 1, d), lambda i, j: (layer, 0, 0)),
            pl.BlockSpec((None, d, tn), lambda i, j: (layer, 0, j)),
            pl.BlockSpec((None, d, tn), lambda i, j: (layer, 0, n_tiles + j)),
        ],
        out_specs=pl.BlockSpec((tm, tn), lambda i, j: (i, j)),
        out_shape=jax.ShapeDtypeStruct((t, d_ff), _BF16),
        scratch_shapes=[pltpu.VMEM((tm, d), _BF16)],
        compiler_params=_params("parallel", "arbitrary"),
        name="ffn_in",
    )(x, gain, w, w)


def _final_norm_kernel(x_ref, g_ref, o_ref):
    o_ref[...] = _rms(x_ref[...], g_ref[...])


def _final_norm(x, gain, row_start, n_rows):
    d = x.shape[1]
    tm = _tile(math.gcd(row_start, n_rows) if row_start else n_rows, 512, 8)
    first = row_start // tm
    return pl.pallas_call(
        _final_norm_kernel,
        grid=(n_rows // tm,),
        in_specs=[pl.BlockSpec((tm, d), lambda i: (first + i, 0)), pl.BlockSpec((1, d), lambda i: (0, 0))],
        out_specs=pl.BlockSpec((tm, d), lambda i: (i, 0)),
        out_shape=jax.ShapeDtypeStruct((n_rows, d), _F32),
        compiler_params=_params("parallel"),
        name="final_norm",
    )(x, gain)


def _rope_tables(seq, hd):
    axis_dim = hd // 2
    t = jnp.arange(seq, dtype=jnp.int32)
    row_idx = (t // _GRID_W).astype(_F32)
    col_idx = (t % _GRID_W).astype(_F32)
    inv_freq = 1.0 / (_ROPE_THETA ** (jnp.arange(0, axis_dim, 2, dtype=_F32) / axis_dim))
    ang_row = row_idx[:, None] * inv_freq[None, :]
    ang_col = col_idx[:, None] * inv_freq[None, :]
    cos = jnp.concatenate([jnp.cos(ang_row)] * 2 + [jnp.cos(ang_col)] * 2, axis=-1)
    sin = jnp.concatenate([-jnp.sin(ang_row), jnp.sin(ang_row), -jnp.sin(ang_col), jnp.sin(ang_col)], axis=-1)
    return cos, sin


def _with_partner_gain(gain, hd):
    return jnp.stack([gain, gain[:, _rotary_partner(hd)]], axis=1)


def _trunk(x, p, *, seq):
    d = x.shape[1]
    depth = p["w_in"].shape[0]
    d_conv = p["w_br_conv"].shape[1]
    d_sgu = p["w_br_sgu"].shape[1]
    d_attn = p["w_br_attn"].shape[1]
    hd = p["q_gain"].shape[2]
    n_branch = p["gate_bias"].shape[1]
    d_in = p["w_in"].shape[2]
    d_kv = (d_in - 3 * d_conv - 2 * d_sgu - d_attn - n_branch * d) // 2
    n_q, n_kv = d_attn // hd, d_kv // hd
    q_off = 3 * d_conv + 2 * d_sgu
    k_off = q_off + d_attn
    v_off = k_off + d_kv
    g_off = v_off + d_kv
    tk = _tile(seq, 512, 128)
    tq = _tile(seq, 512, 16)
    cos, sin = _rope_tables(seq, hd)
    bound = _score_bound(p["q_gain"], p["k_gain"], hd)

    for l in range(depth):
        z = _in_proj(x, p["norm_mix"], p["w_in"], l)
        q, k, vt = _qkv_prep(z, p["q_gain"], p["k_gain"], cos, sin, l, seq=seq, q_off=q_off, k_off=k_off,
                             v_off=v_off, n_q=n_q, n_kv=n_kv, hd=hd, tk=tk)
        attn = _attention(bound, q, k, vt, l, seq=seq, hd=hd, tq=tq)
        conv, sgu = _mixers(z, p["conv_w"], p["sgu_ln"], p["sgu_ws"], p["sgu_bias"], l,
                            seq=seq, d_conv=d_conv, d_sgu=d_sgu)
        merged = _branch_merge(conv, sgu, attn, z, p["gate_bias"], p["w_br_conv"], p["w_br_sgu"],
                               p["w_br_attn"], l, g_off=g_off)
        x = _matmul_residual(merged, p["w_out"], x, l, tn_pref=1024, name="out_proj")
        act = _ffn_in(x, p["norm_ffn"], p["w_ffn_in"], l)
        x = _matmul_residual(act, p["w_ffn_out"], x, l, tn_pref=512, name="ffn_out")
    return x


def kernel(x_prompt, x_sample, norm_mix, w_in, gate_bias, conv_w, sgu_ln, sgu_ws, sgu_b, q_norm, k_norm,
           w_br_conv, w_br_sgu, w_br_attn, w_out, norm_ffn, w_ffn_in, w_ffn_out, norm_final):
    d = x_prompt.shape[-1]
    hd = q_norm.shape[1]
    gd = w_br_sgu.shape[1] // sgu_ws.shape[1]
    p = {
        "norm_mix": norm_mix[:, None, :],
        "norm_ffn": norm_ffn[:, None, :],
        "q_gain": _with_partner_gain(q_norm * ((hd ** -0.5) * math.log2(math.e)), hd),
        "k_gain": _with_partner_gain(k_norm, hd),
        "sgu_ln": sgu_ln[:, None, :],
        "gate_bias": gate_bias,
        "conv_w": conv_w,
        "sgu_bias": jnp.repeat(jnp.swapaxes(sgu_b, 1, 2), gd, axis=2),
        "sgu_ws": sgu_ws.astype(_BF16),
        "w_in": w_in.astype(_BF16),
        "w_br_conv": w_br_conv.astype(_BF16),
        "w_br_sgu": w_br_sgu.astype(_BF16),
        "w_br_attn": w_br_attn.astype(_BF16),
        "w_out": w_out.astype(_BF16),
        "w_ffn_in": w_ffn_in.astype(_BF16),
        "w_ffn_out": w_ffn_out.astype(_BF16),
    }
    gain_final = norm_final[None, :]

    def run(xs):
        seq = xs[0].shape[1]
        x = jnp.concatenate([a.reshape(-1, d) for a in xs], axis=0) if len(xs) > 1 else xs[0].reshape(-1, d)
        x = _trunk(x, p, seq=seq)
        outs, start = [], 0
        for a in xs:
            rows = a.shape[0] * a.shape[1]
            outs.append(_final_norm(x, gain_final, start, rows).reshape(a.shape))
            start += rows
        return outs

    if x_prompt.shape[1] == x_sample.shape[1]:
        y_prompt, y_sample = run([x_prompt, x_sample])
    else:
        (y_prompt,), (y_sample,) = run([x_prompt]), run([x_sample])
    return (y_prompt, y_sample)
```
